```python
import math
import jax, jax.numpy as jnp
from jax import lax
import numpy as np

D_MODEL = 1024
BATCH = 8
SEQ = 2048
DEPTH = 4
DEC_BATCH = 128
DEC_SEQ = 4
PAST_LEN = 16384
PAGE_SIZE = 128

N_EVEN = (DEPTH + 1) // 2
N_ODD = DEPTH // 2
D_A = D_MODEL // 2
H_A = 8
BLK_A = D_A // H_A
CONV_A = 4
RG_C = 8.0
D_B = D_MODEL // 2
H_B = 4
DK = D_B // H_B
DV = D_B // H_B
CONV_B = 4
MLSTM_CHUNK = 64
D_IN_EVEN = 2 * D_A + 4 * D_B + 2 * H_B
D_C = D_MODEL
CONV_C = 31
D_FF = 2816
N_EXPERTS = 8
TOP_K = 2
D_FF_E = 2816
MOE_BLOCK = 256
EPS = 1e-6

kernel_name = 'hawk_mlstm_conformer_moe_step'

EVEN_KEYS = ('state_rglru_h', 'cache_rglru_conv', 'state_mlstm_c', 'state_mlstm_n',
             'state_mlstm_m', 'cache_mlstm_conv')
ODD_KEYS = ('cache_conformer_conv',)
STATE_KEYS = EVEN_KEYS + ODD_KEYS


def _rmsnorm(x, g):
    xf = x.astype(jnp.float32)
    y = xf * lax.rsqrt(jnp.mean(xf * xf, axis=-1, keepdims=True) + EPS)
    return (y * g.astype(jnp.float32)).astype(x.dtype)


def _layernorm_f32(x, g, b):
    mu = jnp.mean(x, axis=-1, keepdims=True)
    var = jnp.mean(jnp.square(x - mu), axis=-1, keepdims=True)
    return (x - mu) * lax.rsqrt(var + EPS) * g + b


def _modulate(h, shift, scale):
    return h * (1 + scale[:, None, :]) + shift[:, None, :]


def _causal_dwconv(x, buf, w, b):
    xp = jnp.concatenate([buf.astype(x.dtype), x], axis=1)
    y = lax.conv_general_dilated(xp, w[:, None, :].astype(x.dtype), (1,), 'VALID',
                                 dimension_numbers=('NWC', 'WIO', 'NWC'),
                                 feature_group_count=x.shape[-1])
    return y + b.astype(x.dtype), xp[:, -(w.shape[0] - 1):]


def _rglru(xc, wa, ba, wx, bx, lam, h0, seq_start):
    B, T, _ = xc.shape
    xf = xc.astype(jnp.float32)
    xb = xf.reshape(B, T, H_A, BLK_A)
    r = jax.nn.sigmoid(jnp.einsum('bthi,hij->bthj', xb, wa.astype(jnp.float32)).reshape(B, T, D_A)
                       + ba.astype(jnp.float32))
    i = jax.nn.sigmoid(jnp.einsum('bthi,hij->bthj', xb, wx.astype(jnp.float32)).reshape(B, T, D_A)
                       + bx.astype(jnp.float32))
    log_a = -RG_C * r * jax.nn.softplus(-lam.astype(jnp.float32))
    a = jnp.exp(log_a)
    mult = jnp.sqrt(-jnp.expm1(2.0 * log_a))
    if seq_start:
        mult = mult.at[:, 0].set(1.0)
    bterm = mult * (i * xf)
    bterm = bterm.at[:, 0].add(a[:, 0] * h0.astype(jnp.float32))

    def combine(lhs, rhs):
        return (lhs[0] * rhs[0], rhs[0] * lhs[1] + rhs[1])

    _, h = lax.associative_scan(combine, (a, bterm), axis=1)
    return h, h[:, -1]


def _mlstm(q, k, v, i_pre, log_f, c0, n0, m0):
    B, T, H, _ = q.shape
    L = math.gcd(T, MLSTM_CHUNK)
    nc = T // L

    def chunks(a):
        a = a.reshape((B, nc, L, H) + a.shape[3:])
        return jnp.moveaxis(jnp.moveaxis(a, 1, 0), 3, 2)

    tri = jnp.tril(jnp.ones((L, L), bool))

    def step(carry, inp):
        c, n, m = carry
        qc, kc, vc, ic, fc = inp
        b = jnp.cumsum(fc, axis=-1)
        logd = jnp.where(tri, b[..., :, None] - b[..., None, :] + ic[..., None, :], -jnp.inf)
        log_s = b + m[..., None]
        mt = jnp.maximum(log_s, jnp.max(logd, axis=-1))
        dmat = jnp.exp(logd - mt[..., None])
        ws = jnp.exp(log_s - mt)
        s = jnp.einsum('bhtd,bhsd->bhts', qc, kc) * dmat
        num = jnp.einsum('bhts,bhsv->bhtv', s, vc) + ws[..., None] * jnp.einsum('bhtd,bhdv->bhtv', qc, c)
        den = jnp.sum(s, axis=-1) + ws * jnp.einsum('bhtd,bhd->bht', qc, n)
        hc = num / jnp.maximum(jnp.abs(den), jnp.exp(-mt))[..., None]
        m_new = mt[..., -1]
        wk = jnp.exp(b[..., -1:] - b + ic - m_new[..., None])
        decay = jnp.exp(b[..., -1] + m - m_new)
        c_new = decay[..., None, None] * c + jnp.einsum('bhs,bhsd,bhsv->bhdv', wk, kc, vc)
        n_new = decay[..., None] * n + jnp.einsum('bhs,bhsd->bhd', wk, kc)
        return (c_new, n_new, m_new), hc

    carry0 = (c0.astype(jnp.float32), n0.astype(jnp.float32), m0.astype(jnp.float32))
    (c, n, m), hs = lax.scan(step, carry0, (chunks(q), chunks(k), chunks(v), chunks(i_pre), chunks(log_f)))
    h = jnp.moveaxis(jnp.moveaxis(hs, 2, 3), 0, 1).reshape(B, T, H, v.shape[-1])
    return h, c, n, m


def _even_mixer(h, st, P, e, seq_start):
    B, T, _ = h.shape
    z = h @ P['w_in_even'][e]
    xa, ga, qk, v, o, gates = jnp.split(
        z, [D_A, 2 * D_A, 2 * D_A + 2 * D_B, 2 * D_A + 3 * D_B, 2 * D_A + 4 * D_B], axis=-1)
    xc, rconv = _causal_dwconv(xa, st['cache_rglru_conv'], P['rglru_conv_w'][e], P['rglru_conv_b'][e])
    hA, rh = _rglru(xc, P['rglru_wa'][e], P['rglru_ba'][e], P['rglru_wx'][e], P['rglru_bx'][e],
                    P['rglru_lambda'][e], st['state_rglru_h'], seq_start)
    yA = jax.nn.gelu(ga.astype(jnp.float32)) * hA
    qkc, mconv = _causal_dwconv(qk, st['cache_mlstm_conv'], P['mlstm_conv_w'][e], P['mlstm_conv_b'][e])
    qkc = qkc.astype(jnp.float32)
    q = qkc[..., :D_B].reshape(B, T, H_B, DK)
    k = qkc[..., D_B:].reshape(B, T, H_B, DK) * (DK ** -0.5)
    vv = v.astype(jnp.float32).reshape(B, T, H_B, DV)
    gb = P['mlstm_gate_b'][e].astype(jnp.float32)
    gf = gates.astype(jnp.float32)
    i_pre = gf[..., :H_B] + gb[:H_B]
    log_f = jax.nn.log_sigmoid(gf[..., H_B:] + gb[H_B:])
    hB, mc, mn, mm = _mlstm(q, k, vv, i_pre, log_f, st['state_mlstm_c'], st['state_mlstm_n'],
                            st['state_mlstm_m'])
    mu = jnp.mean(hB, axis=-1, keepdims=True)
    var = jnp.mean(jnp.square(hB - mu), axis=-1, keepdims=True)
    hB = ((hB - mu) * lax.rsqrt(var + EPS)).reshape(B, T, D_B) * P['mlstm_norm_g'][e].astype(jnp.float32)
    yB = jax.nn.sigmoid(o.astype(jnp.float32)) * hB
    y = jnp.concatenate([yA, yB], axis=-1).astype(h.dtype) @ P['w_out_even'][e]
    dt = h.dtype
    upd = {'state_rglru_h': rh.astype(dt), 'cache_rglru_conv': rconv.astype(dt),
           'state_mlstm_c': mc.astype(dt), 'state_mlstm_n': mn.astype(dt),
           'state_mlstm_m': mm.astype(dt), 'cache_mlstm_conv': mconv.astype(dt)}
    return y, upd


def _conformer_conv(h, buf, P, o):
    u = h @ P['conf_w_pw1'][o] + P['conf_b_pw1'][o]
    u = u[..., :D_C] * jax.nn.sigmoid(u[..., D_C:])
    uc, new_buf = _causal_dwconv(u, buf, P['conf_dw_w'][o], P['conf_dw_b'][o])
    un = _layernorm_f32(uc.astype(jnp.float32), P['conf_ln_g'][o].astype(jnp.float32),
                        P['conf_ln_b'][o].astype(jnp.float32))
    y = jax.nn.silu(un).astype(h.dtype) @ P['conf_w_pw2'][o] + P['conf_b_pw2'][o]
    return y, new_buf.astype(h.dtype)


def _swiglu(h, wg, wu, wd):
    return (jax.nn.silu(h @ wg) * (h @ wu)) @ wd


def _moe_swiglu(h, w_r, b_r, wg, wu, wd):
    B, T, D = h.shape
    hf = h.reshape(-1, D)
    N = hf.shape[0]
    M = N * TOP_K
    logits = hf.astype(jnp.float32) @ w_r.astype(jnp.float32) + b_r.astype(jnp.float32)
    top_v, top_i = lax.top_k(logits, TOP_K)
    gate = jax.nn.softmax(top_v, axis=-1).astype(h.dtype)
    e_flat = top_i.reshape(-1)
    order = jnp.argsort(e_flat)
    e_sorted = e_flat[order]
    counts = jnp.bincount(e_flat, length=N_EXPERTS)
    starts = jnp.cumsum(counts) - counts
    padded = (counts + MOE_BLOCK - 1) // MOE_BLOCK * MOE_BLOCK
    pends = jnp.cumsum(padded)
    pstarts = pends - padded
    dest_sorted = (pstarts[e_sorted] + jnp.arange(M) - starts[e_sorted]).astype(jnp.int32)
    dest = jnp.zeros((M,), jnp.int32).at[order].set(dest_sorted)
    n_blocks = -(-(M + N_EXPERTS * (MOE_BLOCK - 1)) // MOE_BLOCK)
    buf = jnp.zeros((n_blocks * MOE_BLOCK, D), h.dtype).at[dest].set(jnp.repeat(hf, TOP_K, axis=0))
    block_e = jnp.minimum(jnp.searchsorted(pends, jnp.arange(n_blocks) * MOE_BLOCK, side='right'),
                          N_EXPERTS - 1)

    def expert_block(args):
        xb, e = args
        return (jax.nn.silu(xb @ wg[e]) * (xb @ wu[e])) @ wd[e]

    out = lax.map(expert_block, (buf.reshape(n_blocks, MOE_BLOCK, D), block_e)).reshape(-1, D)
    y = jnp.einsum('nkd,nk->nd', out[dest].reshape(N, TOP_K, D), gate)
    return y.reshape(B, T, D)


def _trunk(x, c, st, P, seq_start):
    new = {k: [] for k in STATE_KEYS}
    cond = jax.nn.silu(c)
    for l in range(DEPTH):
        mod = cond @ P['ada_w'][l] + P['ada_b'][l]
        sh1, sc1, g1, sh2, sc2, g2 = jnp.split(mod, 6, axis=-1)
        h = _modulate(_rmsnorm(x, P['norm_g'][l, 0]), sh1, sc1)
        if l % 2 == 0:
            e = l // 2
            y, upd = _even_mixer(h, {k: st[k][e] for k in EVEN_KEYS}, P, e, seq_start)
            for k in EVEN_KEYS:
                new[k].append(upd[k])
        else:
            o = l // 2
            y, nb = _conformer_conv(h, st['cache_conformer_conv'][o], P, o)
            new['cache_conformer_conv'].append(nb)
        x = x + g1[:, None, :] * y
        h = _modulate(_rmsnorm(x, P['norm_g'][l, 1]), sh2, sc2)
        if l % 2 == 0:
            f = _swiglu(h, P['ffn_w_gate'][e], P['ffn_w_up'][e], P['ffn_w_down'][e])
        else:
            f = _moe_swiglu(h, P['moe_router_w'][o], P['moe_router_b'][o], P['moe_w_gate'][o],
                            P['moe_w_up'][o], P['moe_w_down'][o])
        x = x + g2[:, None, :] * f
    return _rmsnorm(x, P['final_norm_g']), {k: jnp.stack(v) for k, v in new.items()}


def _zero_state(b, dt):
    return {'state_rglru_h': jnp.zeros((N_EVEN, b, D_A), dt),
            'cache_rglru_conv': jnp.zeros((N_EVEN, b, CONV_A - 1, D_A), dt),
            'state_mlstm_c': jnp.zeros((N_EVEN, b, H_B, DK, DV), dt),
            'state_mlstm_n': jnp.zeros((N_EVEN, b, H_B, DK), dt),
            'state_mlstm_m': jnp.zeros((N_EVEN, b, H_B), dt),
            'cache_mlstm_conv': jnp.zeros((N_EVEN, b, CONV_B - 1, 2 * D_B), dt),
            'cache_conformer_conv': jnp.zeros((N_ODD, b, CONV_C - 1, D_C), dt)}


def setup_inputs(seed: int = 0) -> dict:
    key = jax.random.key(seed)
    ks = iter(jax.random.split(key, 64))

    def nrm(shape, scale):
        return jax.random.normal(next(ks), shape, jnp.float32) * scale

    inp = {}
    inp['x_prompt'] = nrm((BATCH, SEQ, D_MODEL), 1.0)
    inp['x_sample'] = nrm((DEC_BATCH, DEC_SEQ, D_MODEL), 1.0)
    inp['state_rglru_h'] = nrm((N_EVEN, DEC_BATCH, D_A), 0.5)
    inp['cache_rglru_conv'] = nrm((N_EVEN, DEC_BATCH, CONV_A - 1, D_A), 1.0)
    inp['state_mlstm_c'] = nrm((N_EVEN, DEC_BATCH, H_B, DK, DV), DK ** -0.5)
    inp['state_mlstm_n'] = nrm((N_EVEN, DEC_BATCH, H_B, DK), 1.0)
    inp['state_mlstm_m'] = nrm((N_EVEN, DEC_BATCH, H_B), 0.5)
    inp['cache_mlstm_conv'] = nrm((N_EVEN, DEC_BATCH, CONV_B - 1, 2 * D_B), 1.0)
    inp['cache_conformer_conv'] = nrm((N_ODD, DEC_BATCH, CONV_C - 1, D_C), 1.0)
    inp['c_prompt'] = nrm((BATCH, D_MODEL), 1.0)
    inp['c_sample'] = nrm((DEC_BATCH, D_MODEL), 1.0)
    inp['norm_g'] = 1.0 + nrm((DEPTH, 2, D_MODEL), 0.02)
    inp['ada_w'] = nrm((DEPTH, D_MODEL, 6 * D_MODEL), 0.5 * D_MODEL ** -0.5)
    inp['ada_b'] = nrm((DEPTH, 6 * D_MODEL), 0.02)
    inp['final_norm_g'] = 1.0 + nrm((D_MODEL,), 0.02)
    inp['w_in_even'] = nrm((N_EVEN, D_MODEL, D_IN_EVEN), D_MODEL ** -0.5)
    inp['rglru_conv_w'] = nrm((N_EVEN, CONV_A, D_A), CONV_A ** -0.5)
    inp['rglru_conv_b'] = nrm((N_EVEN, D_A), 0.02)
    inp['rglru_wa'] = nrm((N_EVEN, H_A, BLK_A, BLK_A), BLK_A ** -0.5)
    inp['rglru_ba'] = nrm((N_EVEN, D_A), 0.02)
    inp['rglru_wx'] = nrm((N_EVEN, H_A, BLK_A, BLK_A), BLK_A ** -0.5)
    inp['rglru_bx'] = nrm((N_EVEN, D_A), 0.02)
    u = jax.random.uniform(next(ks), (N_EVEN, D_A), jnp.float32, 0.9, 0.999)
    s = u ** (1.0 / RG_C)
    inp['rglru_lambda'] = jnp.log(s) - jnp.log1p(-s)
    inp['mlstm_conv_w'] = nrm((N_EVEN, CONV_B, 2 * D_B), CONV_B ** -0.5)
    inp['mlstm_conv_b'] = nrm((N_EVEN, 2 * D_B), 0.02)
    f_bias = jnp.broadcast_to(jnp.linspace(3.0, 6.0, H_B), (N_EVEN, H_B)) + nrm((N_EVEN, H_B), 0.1)
    inp['mlstm_gate_b'] = jnp.concatenate([nrm((N_EVEN, H_B), 0.1), f_bias], axis=-1)
    inp['mlstm_norm_g'] = 1.0 + nrm((N_EVEN, D_B), 0.02)
    inp['w_out_even'] = nrm((N_EVEN, D_A + D_B, D_MODEL), (D_A + D_B) ** -0.5)
    inp['ffn_w_gate'] = nrm((N_EVEN, D_MODEL, D_FF), D_MODEL ** -0.5)
    inp['ffn_w_up'] = nrm((N_EVEN, D_MODEL, D_FF), D_MODEL ** -0.5)
    inp['ffn_w_down'] = nrm((N_EVEN, D_FF, D_MODEL), D_FF ** -0.5)
    inp['conf_w_pw1'] = nrm((N_ODD, D_MODEL, 2 * D_C), D_MODEL ** -0.5)
    inp['conf_b_pw1'] = nrm((N_ODD, 2 * D_C), 0.02)
    inp['conf_dw_w'] = nrm((N_ODD, CONV_C, D_C), CONV_C ** -0.5)
    inp['conf_dw_b'] = nrm((N_ODD, D_C), 0.02)
    inp['conf_ln_g'] = 1.0 + nrm((N_ODD, D_C), 0.02)
    inp['conf_ln_b'] = nrm((N_ODD, D_C), 0.02)
    inp['conf_w_pw2'] = nrm((N_ODD, D_C, D_MODEL), D_C ** -0.5)
    inp['conf_b_pw2'] = nrm((N_ODD, D_MODEL), 0.02)
    inp['moe_router_w'] = nrm((N_ODD, D_MODEL, N_EXPERTS), D_MODEL ** -0.5)
    inp['moe_router_b'] = nrm((N_ODD, N_EXPERTS), 0.01)
    inp['moe_w_gate'] = nrm((N_ODD, N_EXPERTS, D_MODEL, D_FF_E), D_MODEL ** -0.5)
    inp['moe_w_up'] = nrm((N_ODD, N_EXPERTS, D_MODEL, D_FF_E), D_MODEL ** -0.5)
    inp['moe_w_down'] = nrm((N_ODD, N_EXPERTS, D_FF_E, D_MODEL), D_FF_E ** -0.5)
    return inp


def reference(x_prompt, x_sample, state_rglru_h, cache_rglru_conv, state_mlstm_c, state_mlstm_n,
              state_mlstm_m, cache_mlstm_conv, cache_conformer_conv, c_prompt, c_sample,
              norm_g, ada_w, ada_b, final_norm_g, w_in_even, rglru_conv_w, rglru_conv_b,
              rglru_wa, rglru_ba, rglru_wx, rglru_bx, rglru_lambda, mlstm_conv_w, mlstm_conv_b,
              mlstm_gate_b, mlstm_norm_g, w_out_even, ffn_w_gate, ffn_w_up, ffn_w_down,
              conf_w_pw1, conf_b_pw1, conf_dw_w, conf_dw_b, conf_ln_g, conf_ln_b, conf_w_pw2,
              conf_b_pw2, moe_router_w, moe_router_b, moe_w_gate, moe_w_up, moe_w_down):
    P = dict(norm_g=norm_g, ada_w=ada_w, ada_b=ada_b, final_norm_g=final_norm_g,
             w_in_even=w_in_even, rglru_conv_w=rglru_conv_w, rglru_conv_b=rglru_conv_b,
             rglru_wa=rglru_wa, rglru_ba=rglru_ba, rglru_wx=rglru_wx, rglru_bx=rglru_bx,
             rglru_lambda=rglru_lambda, mlstm_conv_w=mlstm_conv_w, mlstm_conv_b=mlstm_conv_b,
             mlstm_gate_b=mlstm_gate_b, mlstm_norm_g=mlstm_norm_g, w_out_even=w_out_even,
             ffn_w_gate=ffn_w_gate, ffn_w_up=ffn_w_up, ffn_w_down=ffn_w_down,
             conf_w_pw1=conf_w_pw1, conf_b_pw1=conf_b_pw1, conf_dw_w=conf_dw_w,
             conf_dw_b=conf_dw_b, conf_ln_g=conf_ln_g, conf_ln_b=conf_ln_b,
             conf_w_pw2=conf_w_pw2, conf_b_pw2=conf_b_pw2, moe_router_w=moe_router_w,
             moe_router_b=moe_router_b, moe_w_gate=moe_w_gate, moe_w_up=moe_w_up,
             moe_w_down=moe_w_down)
    sample_state = dict(state_rglru_h=state_rglru_h, cache_rglru_conv=cache_rglru_conv,
                        state_mlstm_c=state_mlstm_c, state_mlstm_n=state_mlstm_n,
                        state_mlstm_m=state_mlstm_m, cache_mlstm_conv=cache_mlstm_conv,
                        cache_conformer_conv=cache_conformer_conv)
    y_prompt, ps = _trunk(x_prompt, c_prompt, _zero_state(x_prompt.shape[0], x_prompt.dtype), P, True)
    y_sample, ss = _trunk(x_sample, c_sample, sample_state, P, False)
    return (y_prompt, y_sample,
            ps['state_rglru_h'], ps['cache_rglru_conv'], ps['state_mlstm_c'], ps['state_mlstm_n'],
            ps['state_mlstm_m'], ps['cache_mlstm_conv'], ps['cache_conformer_conv'],
            ss['state_rglru_h'], ss['cache_rglru_conv'], ss['state_mlstm_c'], ss['state_mlstm_n'],
            ss['state_mlstm_m'], ss['cache_mlstm_conv'], ss['cache_conformer_conv'])
```

```python
import functools
import math

import jax
import jax.numpy as jnp
from jax import lax
from jax.experimental import pallas as pl
from jax.experimental.pallas import tpu as pltpu

F32 = jnp.float32
BF16 = jnp.bfloat16

D = 1024
D_A = 512
H_A = 8
BLK_A = 64
RG_C = 8.0
D_B = 512
H_B = 4
DK = 128
D_Z = 3200
D_FF = 2816
N_EXPERTS = 8
CONV_C = 31
EPS = 1e-6
TM = 512
TF = 1408
MOE_TM = 512
VMEM_LIMIT = 52 * 1024 * 1024
NEG = -1e30


def _cparams(sem):
    return pltpu.CompilerParams(dimension_semantics=sem, vmem_limit_bytes=VMEM_LIMIT)


def _modnorm(x, g, shift, scale):
    ms = jnp.mean(x * x, axis=-1, keepdims=True)
    y = x * lax.rsqrt(ms + EPS) * g
    return y * (1.0 + scale) + shift


def _shift_rows(x, d, fill):
    rolled = pltpu.roll(x, d, axis=0)
    rows = lax.broadcasted_iota(jnp.int32, x.shape, 0)
    return jnp.where(rows >= d, rolled, fill)


def _cumsum_rows(x):
    d = 1
    while d < x.shape[0]:
        x = x + _shift_rows(x, d, 0.0)
        d *= 2
    return x


def _linear_scan_rows(a, b):
    d = 1
    while d < a.shape[0]:
        b = a * _shift_rows(b, d, 0.0) + b
        a = a * _shift_rows(a, d, 1.0)
        d *= 2
    return a, b


def _log_sigmoid(x):
    return jnp.minimum(x, 0.0) - jnp.log1p(jnp.exp(-jnp.abs(x)))


def _expm1(x):
    u = jnp.exp(x)
    um1 = u - 1.0
    return jnp.where(u == 1.0, x, jnp.where(um1 == -1.0, -1.0, um1 * x / jnp.log(u)))


def _softplus(x):
    return jnp.maximum(x, 0.0) + jnp.log1p(jnp.exp(-jnp.abs(x)))


def _ada_kernel(c_ref, w_ref, b_ref, o_ref):
    c = c_ref[...]
    cond = (c * jax.nn.sigmoid(c)).astype(BF16)
    o_ref[...] = jnp.dot(cond, w_ref[...].astype(BF16), preferred_element_type=F32) + b_ref[...]


def _ada(c_all, ada_w, ada_b):
    depth = ada_w.shape[0]
    n = c_all.shape[0]
    tn = 1536
    return pl.pallas_call(
        _ada_kernel,
        out_shape=jax.ShapeDtypeStruct((depth, n, 6 * D), F32),
        grid=(depth, 6 * D // tn),
        in_specs=[pl.BlockSpec((n, D), lambda l, j: (0, 0)),
                  pl.BlockSpec((None, D, tn), lambda l, j: (l, 0, j)),
                  pl.BlockSpec((None, 1, tn), lambda l, j: (l, 0, j))],
        out_specs=pl.BlockSpec((None, n, tn), lambda l, j: (l, 0, j)),
        compiler_params=_cparams(("arbitrary", "arbitrary")),
        name="ada",
    )(c_all, ada_w, ada_b.reshape(depth, 1, 6 * D))


class _Mods:
    def __init__(self, mod_a, mod_b, tiles_per_seq, p_tiles, n_seq):
        self.a, self.b = mod_a, mod_b
        self.tps, self.pt, self.nb = tiles_per_seq, p_tiles, n_seq

    def specs(self, chunk):
        tps, pt, nb = self.tps, self.pt, self.nb
        sa = pl.BlockSpec((None, 1, D), lambda i, *_: (jnp.minimum(i // tps, nb), 0, chunk))
        sb = pl.BlockSpec((None, TM, D), lambda i, *_: (i // pt, 0, chunk))
        return [sa, sb]

    def args(self):
        return [self.a, self.b]


def _inproj_kernel(x_ref, g_ref, sha, shb, sca, scb, w_ref, b_ref, o_ref, *, glu):
    h = _modnorm(x_ref[...], g_ref[...], sha[...] + shb[...], sca[...] + scb[...]).astype(BF16)
    z = jnp.dot(h, w_ref[...], preferred_element_type=F32) + b_ref[...]
    if glu:
        half = z.shape[1] // 2
        z = z[:, :half] * jax.nn.sigmoid(z[:, half:])
    o_ref[...] = z


def _inproj(x, g, mods, w, b, glu):
    n, nw = x.shape[0], w.shape[1]
    n_out = nw // 2 if glu else nw
    row = lambda i: (i, 0)
    fix = lambda i: (0, 0)
    return pl.pallas_call(
        functools.partial(_inproj_kernel, glu=glu),
        out_shape=jax.ShapeDtypeStruct((n, n_out), F32),
        grid=(n // TM,),
        in_specs=[pl.BlockSpec((TM, D), row), pl.BlockSpec((1, D), fix)]
                 + mods.specs(0) + mods.specs(1)
                 + [pl.BlockSpec((D, nw), fix), pl.BlockSpec((1, nw), fix)],
        out_specs=pl.BlockSpec((TM, n_out), row),
        compiler_params=_cparams(("arbitrary",)),
        name="inproj_glu" if glu else "inproj",
    )(x, g, *mods.args(), *mods.args(), w, b)


def _outproj_kernel(*refs, n_parts, p_tiles):
    parts = refs[:2 * n_parts]
    x_ref, ga, gb, w_ref, b_ref, o_ref = refs[2 * n_parts:]
    i = pl.program_id(0)
    y = b_ref[...]
    k0 = 0
    for p in range(n_parts):
        a = jnp.where(i < p_tiles, parts[2 * p][...], parts[2 * p + 1][...]).astype(BF16)
        kw = a.shape[1]
        y = y + jnp.dot(a, w_ref[k0:k0 + kw, :], preferred_element_type=F32)
        k0 += kw
    o_ref[...] = x_ref[...] + (ga[...] + gb[...]) * y


def _outproj(parts, x, mods, gate_chunk, w, b, p_tiles):
    n = x.shape[0]
    row = lambda i: (i, 0)
    fix = lambda i: (0, 0)
    part_specs, part_args = [], []
    for ap, a_s in parts:
        kw = ap.shape[1]
        part_specs += [pl.BlockSpec((TM, kw), lambda i: (jnp.minimum(i, p_tiles - 1), 0)),
                       pl.BlockSpec((TM, kw), fix)]
        part_args += [ap, a_s]
    return pl.pallas_call(
        functools.partial(_outproj_kernel, n_parts=len(parts), p_tiles=p_tiles),
        out_shape=jax.ShapeDtypeStruct((n, D), F32),
        grid=(n // TM,),
        in_specs=part_specs + [pl.BlockSpec((TM, D), row)] + mods.specs(gate_chunk)
                 + [pl.BlockSpec(w.shape, fix), pl.BlockSpec((1, D), fix)],
        out_specs=pl.BlockSpec((TM, D), row),
        compiler_params=_cparams(("arbitrary",)),
        name="outproj",
    )(*part_args, x, *mods.args(), w, b)


def _ffn_kernel(x_ref, g_ref, sha, shb, sca, scb, ga, gb, wg_ref, wu_ref, wd_ref, o_ref,
                h_ref, acc_ref):
    k = pl.program_id(1)

    @pl.when(k == 0)
    def _():
        h_ref[...] = _modnorm(x_ref[...], g_ref[...], sha[...] + shb[...],
                              sca[...] + scb[...]).astype(BF16)
        acc_ref[...] = jnp.zeros_like(acc_ref)

    h = h_ref[...]
    gt = jnp.dot(h, wg_ref[...], preferred_element_type=F32)
    up = jnp.dot(h, wu_ref[...], preferred_element_type=F32)
    a = (gt * jax.nn.sigmoid(gt) * up).astype(BF16)
    acc_ref[...] += jnp.dot(a, wd_ref[...], preferred_element_type=F32)

    @pl.when(k == pl.num_programs(1) - 1)
    def _():
        o_ref[...] = x_ref[...] + (ga[...] + gb[...]) * acc_ref[...]


def _ffn(x, g, mods, wg, wu, wd):
    n = x.shape[0]
    row = lambda i, k: (i, 0)
    fix = lambda i, k: (0, 0)
    return pl.pallas_call(
        _ffn_kernel,
        out_shape=jax.ShapeDtypeStruct((n, D), F32),
        grid=(n // TM, D_FF // TF),
        in_specs=[pl.BlockSpec((TM, D), row), pl.BlockSpec((1, D), fix)]
                 + mods.specs(3) + mods.specs(4) + mods.specs(5)
                 + [pl.BlockSpec((D, TF), lambda i, k: (0, k)),
                    pl.BlockSpec((D, TF), lambda i, k: (0, k)),
                    pl.BlockSpec((TF, D), lambda i, k: (k, 0))],
        out_specs=pl.BlockSpec((TM, D), row),
        scratch_shapes=[pltpu.VMEM((TM, D), BF16), pltpu.VMEM((TM, D), F32)],
        compiler_params=_cparams(("arbitrary", "arbitrary")),
        name="ffn",
    )(x, g, *mods.args(), *mods.args(), *mods.args(), wg, wu, wd)


def _rglru_kernel(xa_ref, ga_ref, cache0_ref, h0_ref, cw_ref, cb_ref, wcat_ref, bcat_ref,
                  lam_ref, y_ref, hlast_ref, cache_ref, xbuf, hc, *, rows, tv, seq_start):
    ci = pl.program_id(1)

    @pl.when(ci == 0)
    def _():
        xbuf[5:8, :] = cache0_ref[...]
        hc[...] = h0_ref[...]

    xbuf[8:8 + rows, :] = xa_ref[...]
    xc = cb_ref[...]
    for j in range(4):
        xc = xc + cw_ref[j:j + 1, :] * xbuf[5 + j:5 + j + rows, :]
    new_cache = xbuf[5 + tv:8 + tv, :]
    cache_ref[...] = new_cache
    xbuf[5:8, :] = new_cache

    ri = jnp.dot(xc.astype(BF16), wcat_ref[...], preferred_element_type=F32) + bcat_ref[...]
    r = jax.nn.sigmoid(ri[:, :D_A])
    gate_i = jax.nn.sigmoid(ri[:, D_A:])
    log_a = (-RG_C) * r * _softplus(-lam_ref[...])
    a = jnp.exp(log_a)
    mult = jnp.sqrt(-_expm1(2.0 * log_a))
    if seq_start:
        row_id = lax.broadcasted_iota(jnp.int32, mult.shape, 0)
        mult = jnp.where((row_id == 0) & (ci == 0), 1.0, mult)
    a_cum, h_loc = _linear_scan_rows(a, mult * (gate_i * xc))
    h = a_cum * hc[...] + h_loc
    h_end = h[tv - 1:tv, :]
    hc[...] = h_end
    hlast_ref[...] = h_end
    y_ref[...] = jax.nn.gelu(ga_ref[...]) * h


def _rglru(z, layout, cache0, h0, cw, cb, wcat, bcat, lam, seq_start):
    nb, nc, rows, tv = layout["nb"], layout["nc"], layout["rows"], layout["tv"]
    fix = lambda b, c: (0, 0)
    per_b = lambda b, c: (b, 0, 0)
    y, hlast, cache = pl.pallas_call(
        functools.partial(_rglru_kernel, rows=rows, tv=tv, seq_start=seq_start),
        out_shape=[layout["out"](D_A, F32),
                   jax.ShapeDtypeStruct((nb, 1, D_A), F32),
                   jax.ShapeDtypeStruct((nb, 3, D_A), F32)],
        grid=(nb, nc),
        in_specs=[layout["spec"](D_A, 0), layout["spec"](D_A, 1),
                  pl.BlockSpec((None, 3, D_A), per_b), pl.BlockSpec((None, 1, D_A), per_b),
                  pl.BlockSpec((4, D_A), fix), pl.BlockSpec((1, D_A), fix),
                  pl.BlockSpec((D_A, 2 * D_A), fix), pl.BlockSpec((1, 2 * D_A), fix),
                  pl.BlockSpec((1, D_A), fix)],
        out_specs=[layout["spec"](D_A, 0), pl.BlockSpec((None, 1, D_A), per_b),
                   pl.BlockSpec((None, 3, D_A), per_b)],
        scratch_shapes=[pltpu.VMEM((rows + 8, D_A), F32), pltpu.VMEM((1, D_A), F32)],
        compiler_params=_cparams(("arbitrary", "arbitrary")),
        name="rglru",
    )(z, z, cache0, h0, cw, cb, wcat, bcat, lam)
    return y, hlast, cache


def _mlstm_kernel(qk_ref, v_ref, o_ref, gt_ref, cache0_ref, c0_ref, n0_ref, m0_ref,
                  cw_ref, cb_ref, gb_ref, ng_ref,
                  y_ref, c_ref, n_ref, m_ref, cache_ref, xbuf, *, rows, tv):
    ci = pl.program_id(1)

    @pl.when(ci == 0)
    def _():
        xbuf[5:8, :] = cache0_ref[...]
        c_ref[...] = c0_ref[...]
        n_ref[...] = n0_ref[...]
        m_ref[...] = m0_ref[...]

    xbuf[8:8 + rows, :] = qk_ref[...]
    qkc = cb_ref[...]
    for j in range(4):
        qkc = qkc + cw_ref[j:j + 1, :] * xbuf[5 + j:5 + j + rows, :]
    new_cache = xbuf[5 + tv:8 + tv, :]
    cache_ref[...] = new_cache
    xbuf[5:8, :] = new_cache

    g = gt_ref[...] + gb_ref[...]
    lane = lax.broadcasted_iota(jnp.int32, g.shape, 1)
    bcum = _cumsum_rows(_log_sigmoid(g))
    gmix = jnp.where(lane < H_B, g, bcum)
    m_all = m_ref[...]
    lane1 = lax.broadcasted_iota(jnp.int32, m_all.shape, 1)
    row_i = lax.broadcasted_iota(jnp.int32, (rows, rows), 0)
    col_i = lax.broadcasted_iota(jnp.int32, (rows, rows), 1)
    causal = col_i <= row_i
    row_c = lax.broadcasted_iota(jnp.int32, (rows, 1), 0)
    m_out = m_all
    for h in range(H_B):
        sel = jnp.where(lane == h, 1.0, jnp.where(lane == H_B + h, -1.0, 0.0))
        r_mat = lax.dot_general(sel, gmix, (((1,), (1,)), ((), ())),
                                precision=lax.Precision.HIGHEST, preferred_element_type=F32)
        b_col = bcum[:, H_B + h:H_B + h + 1]
        i_col = g[:, h:h + 1]
        m_prev = m_all[:, h:h + 1]
        logd = jnp.where(causal, b_col + r_mat, -jnp.inf)
        log_s = b_col + m_prev
        mt = jnp.maximum(log_s, jnp.max(logd, axis=1, keepdims=True))
        dmat = jnp.exp(logd - mt)
        ws = jnp.exp(log_s - mt)
        q = qkc[:, h * DK:(h + 1) * DK]
        k = qkc[:, D_B + h * DK:D_B + (h + 1) * DK] * (DK ** -0.5)
        vb = v_ref[:, h * DK:(h + 1) * DK].astype(BF16)
        qb = q.astype(BF16)
        s = lax.dot_general(qb, k.astype(BF16), (((1,), (1,)), ((), ())),
                            preferred_element_type=F32) * dmat
        c_h = c_ref[h]
        n_h = n_ref[h:h + 1, :]
        num = (jnp.dot(s.astype(BF16), vb, preferred_element_type=F32)
               + ws * jnp.dot(qb, c_h.astype(BF16), preferred_element_type=F32))
        den = (jnp.sum(s, axis=1, keepdims=True)
               + ws * jnp.sum(q * n_h, axis=1, keepdims=True))
        hc = num / jnp.maximum(jnp.abs(den), jnp.exp(-mt))
        m_new = mt[tv - 1:tv, :]
        b_last = b_col[tv - 1:tv, :]
        wk = jnp.exp(b_last - b_col + i_col - m_new)
        if tv < rows:
            wk = jnp.where(row_c < tv, wk, 0.0)
        decay = jnp.exp(b_last + m_prev - m_new)
        kw = k * wk
        c_ref[h] = decay * c_h + lax.dot_general(kw.astype(BF16), vb, (((0,), (0,)), ((), ())),
                                                 preferred_element_type=F32)
        n_ref[h:h + 1, :] = decay * n_h + jnp.sum(kw, axis=0, keepdims=True)
        m_out = jnp.where(lane1 == h, m_new, m_out)
        mu = jnp.mean(hc, axis=1, keepdims=True)
        dev = hc - mu
        var = jnp.mean(dev * dev, axis=1, keepdims=True)
        hn = dev * lax.rsqrt(var + EPS) * ng_ref[:, h * DK:(h + 1) * DK]
        y_ref[:, h * DK:(h + 1) * DK] = jax.nn.sigmoid(o_ref[:, h * DK:(h + 1) * DK]) * hn
    m_ref[...] = m_out


def _mlstm(z, layout, cache0, c0, n0, m0, cw, cb, gb, ng):
    nb, nc, rows, tv = layout["nb"], layout["nc"], layout["rows"], layout["tv"]
    fix = lambda b, c: (0, 0)
    per_b = lambda b, c: (b, 0, 0)
    per_b4 = lambda b, c: (b, 0, 0, 0)
    return pl.pallas_call(
        functools.partial(_mlstm_kernel, rows=rows, tv=tv),
        out_shape=[layout["out"](D_B, F32),
                   jax.ShapeDtypeStruct((nb, H_B, DK, DK), F32),
                   jax.ShapeDtypeStruct((nb, H_B, DK), F32),
                   jax.ShapeDtypeStruct((nb, 1, 128), F32),
                   jax.ShapeDtypeStruct((nb, 3, 2 * D_B), F32)],
        grid=(nb, nc),
        in_specs=[layout["spec"](2 * D_B, 1), layout["spec"](D_B, 4), layout["spec"](D_B, 5),
                  layout["spec"](128, 24),
                  pl.BlockSpec((None, 3, 2 * D_B), per_b),
                  pl.BlockSpec((None, H_B, DK, DK), per_b4),
                  pl.BlockSpec((None, H_B, DK), per_b),
                  pl.BlockSpec((None, 1, 128), per_b),
                  pl.BlockSpec((4, 2 * D_B), fix), pl.BlockSpec((1, 2 * D_B), fix),
                  pl.BlockSpec((1, 128), fix), pl.BlockSpec((1, D_B), fix)],
        out_specs=[layout["spec"](D_B, 0),
                   pl.BlockSpec((None, H_B, DK, DK), per_b4),
                   pl.BlockSpec((None, H_B, DK), per_b),
                   pl.BlockSpec((None, 1, 128), per_b),
                   pl.BlockSpec((None, 3, 2 * D_B), per_b)],
        scratch_shapes=[pltpu.VMEM((rows + 8, 2 * D_B), F32)],
        compiler_params=_cparams(("arbitrary", "arbitrary")),
        name="mlstm",
    )(z, z, z, z, cache0, c0, n0, m0, cw, cb, gb, ng)


def _conf_kernel(u_ref, cache0_ref, w_ref, b_ref, g_ref, be_ref, o_ref, cache_ref, xbuf,
                 *, rows, tv):
    ci = pl.program_id(1)
    hist = CONV_C - 1

    @pl.when(ci == 0)
    def _():
        xbuf[2:2 + hist, :] = cache0_ref[...]

    xbuf[32:32 + rows, :] = u_ref[...]
    acc = b_ref[...]
    for j in range(CONV_C):
        acc = acc + w_ref[j:j + 1, :] * xbuf[2 + j:2 + j + rows, :]
    new_cache = xbuf[2 + tv:2 + tv + hist, :]
    cache_ref[...] = new_cache
    xbuf[2:2 + hist, :] = new_cache
    mu = jnp.mean(acc, axis=1, keepdims=True)
    dev = acc - mu
    var = jnp.mean(dev * dev, axis=1, keepdims=True)
    un = dev * lax.rsqrt(var + EPS) * g_ref[...] + be_ref[...]
    o_ref[...] = (un * jax.nn.sigmoid(un)).astype(o_ref.dtype)


def _conf(u, layout, cache0, w, b, g, be):
    nb, nc, rows, tv = layout["nb"], layout["nc"], layout["rows"], layout["tv"]
    hist = CONV_C - 1
    fix = lambda b_, c: (0, 0)
    per_b = lambda b_, c: (b_, 0, 0)
    return pl.pallas_call(
        functools.partial(_conf_kernel, rows=rows, tv=tv),
        out_shape=[layout["out"](D, BF16), jax.ShapeDtypeStruct((nb, hist, D), F32)],
        grid=(nb, nc),
        in_specs=[layout["spec"](D, 0), pl.BlockSpec((None, hist, D), per_b),
                  pl.BlockSpec((CONV_C, D), fix), pl.BlockSpec((1, D), fix),
                  pl.BlockSpec((1, D), fix), pl.BlockSpec((1, D), fix)],
        out_specs=[layout["spec"](D, 0), pl.BlockSpec((None, hist, D), per_b)],
        scratch_shapes=[pltpu.VMEM((rows + 32, D), F32)],
        compiler_params=_cparams(("arbitrary", "arbitrary")),
        name="conformer_conv",
    )(u, cache0, w, b, g, be)


def _prompt_layout(n_seq, seq_len, rows):
    nc = seq_len // rows

    def spec(width, colblk):
        return pl.BlockSpec((rows, width), lambda b, c: (b * nc + c, colblk))

    def out(width, dtype):
        return jax.ShapeDtypeStruct((n_seq * seq_len, width), dtype)

    return dict(nb=n_seq, nc=nc, rows=rows, tv=rows, spec=spec, out=out)


def _sample_layout(n_seq, seq_len, rows):
    def spec(width, colblk):
        return pl.BlockSpec((None, rows, width), lambda b, c: (b, 0, colblk))

    def out(width, dtype):
        return jax.ShapeDtypeStruct((n_seq, rows, width), dtype)

    return dict(nb=n_seq, nc=1, rows=rows, tv=seq_len, spec=spec, out=out)


def _router_kernel(x_ref, g_ref, sha, shb, sca, scb, wr_ref, br_ref, h_ref, r_ref):
    h = _modnorm(x_ref[...], g_ref[...], sha[...] + shb[...], sca[...] + scb[...])
    h_ref[...] = h.astype(h_ref.dtype)
    logits = jnp.dot(h, wr_ref[...], precision=lax.Precision.HIGHEST,
                     preferred_element_type=F32) + br_ref[...]
    lane = lax.broadcasted_iota(jnp.int32, logits.shape, 1)
    logits = jnp.where(lane < N_EXPERTS, logits, -jnp.inf)
    m1 = jnp.max(logits, axis=1, keepdims=True)
    i1 = jnp.min(jnp.where(logits == m1, lane, 128), axis=1, keepdims=True)
    rest = jnp.where(lane == i1, -jnp.inf, logits)
    m2 = jnp.max(rest, axis=1, keepdims=True)
    i2 = jnp.min(jnp.where(rest == m2, lane, 128), axis=1, keepdims=True)
    e2 = jnp.exp(m2 - m1)
    g1 = 1.0 / (1.0 + e2)
    g2 = e2 / (1.0 + e2)
    r_ref[...] = jnp.where(lane == 0, i1.astype(F32),
                           jnp.where(lane == 1, i2.astype(F32),
                                     jnp.where(lane == 2, g1, jnp.where(lane == 3, g2, 0.0))))


def _router(x, g, mods, wr, br):
    n = x.shape[0]
    row = lambda i: (i, 0)
    fix = lambda i: (0, 0)
    return pl.pallas_call(
        _router_kernel,
        out_shape=[jax.ShapeDtypeStruct((n, D), BF16), jax.ShapeDtypeStruct((n, 128), F32)],
        grid=(n // TM,),
        in_specs=[pl.BlockSpec((TM, D), row), pl.BlockSpec((1, D), fix)]
                 + mods.specs(3) + mods.specs(4)
                 + [pl.BlockSpec((D, 128), fix), pl.BlockSpec((1, 128), fix)],
        out_specs=[pl.BlockSpec((TM, D), row), pl.BlockSpec((TM, 128), row)],
        compiler_params=_cparams(("arbitrary",)),
        name="router",
    )(x, g, *mods.args(), *mods.args(), wr, br)


def _moe_ffn_kernel(be_ref, nv_ref, x_ref, wg_ref, wu_ref, wd_ref, o_ref, acc_ref):
    i = pl.program_id(0)
    k = pl.program_id(1)
    last = pl.num_programs(1) - 1

    @pl.when(i < nv_ref[0])
    def _():
        @pl.when(k == 0)
        def _():
            acc_ref[...] = jnp.zeros_like(acc_ref)

        x = x_ref[...]
        gt = jnp.dot(x, wg_ref[...], preferred_element_type=F32)
        up = jnp.dot(x, wu_ref[...], preferred_element_type=F32)
        a = (gt * jax.nn.sigmoid(gt) * up).astype(BF16)
        acc_ref[...] += jnp.dot(a, wd_ref[...], preferred_element_type=F32)

        @pl.when(k == last)
        def _():
            o_ref[...] = acc_ref[...]

    @pl.when((i >= nv_ref[0]) & (k == last))
    def _():
        o_ref[...] = jnp.zeros_like(o_ref)


def _moe_ffn(xs, block_e, n_valid, wg, wu, wd):
    m_pad = xs.shape[0]
    n_blocks = m_pad // MOE_TM
    nk = D_FF // TF

    def kk(i, k, nv):
        return jnp.where(i < nv[0], k, nk - 1)

    grid_spec = pltpu.PrefetchScalarGridSpec(
        num_scalar_prefetch=2,
        grid=(n_blocks, nk),
        in_specs=[pl.BlockSpec((MOE_TM, D), lambda i, k, be, nv: (i, 0)),
                  pl.BlockSpec((None, D, TF), lambda i, k, be, nv: (be[i], 0, kk(i, k, nv))),
                  pl.BlockSpec((None, D, TF), lambda i, k, be, nv: (be[i], 0, kk(i, k, nv))),
                  pl.BlockSpec((None, TF, D), lambda i, k, be, nv: (be[i], kk(i, k, nv), 0))],
        out_specs=pl.BlockSpec((MOE_TM, D), lambda i, k, be, nv: (i, 0)),
        scratch_shapes=[pltpu.VMEM((MOE_TM, D), F32)])
    return pl.pallas_call(
        _moe_ffn_kernel,
        out_shape=jax.ShapeDtypeStruct((m_pad, D), F32),
        grid_spec=grid_spec,
        compiler_params=_cparams(("arbitrary", "arbitrary")),
        name="moe_ffn",
    )(block_e, n_valid, xs, wg, wu, wd)


def _combine_kernel(x_ref, ya_ref, yb_ref, r_ref, ga, gb, fg_ref, o_ref, *, final):
    r = r_ref[...]
    y = r[:, 2:3] * ya_ref[...] + r[:, 3:4] * yb_ref[...]
    x = x_ref[...] + (ga[...] + gb[...]) * y
    if final:
        ms = jnp.mean(x * x, axis=-1, keepdims=True)
        x = x * lax.rsqrt(ms + EPS) * fg_ref[...]
    o_ref[...] = x


def _combine(x, ya, yb, route, mods, fg, final):
    n = x.shape[0]
    row = lambda i: (i, 0)
    fix = lambda i: (0, 0)
    return pl.pallas_call(
        functools.partial(_combine_kernel, final=final),
        out_shape=jax.ShapeDtypeStruct((n, D), F32),
        grid=(n // TM,),
        in_specs=[pl.BlockSpec((TM, D), row), pl.BlockSpec((TM, D), row),
                  pl.BlockSpec((TM, D), row), pl.BlockSpec((TM, 128), row)]
                 + mods.specs(5) + [pl.BlockSpec((1, D), fix)],
        out_specs=pl.BlockSpec((TM, D), row),
        compiler_params=_cparams(("arbitrary",)),
        name="moe_combine",
    )(x, ya, yb, route, *mods.args(), fg)


def _moe(x, g, mods, wr, br, wg, wu, wd, fg, final):
    n = x.shape[0]
    m = 2 * n
    h2, route = _router(x, g, mods, wr, br)
    e_flat = route[:, :2].astype(jnp.int32).reshape(-1)
    onehot = (e_flat[:, None] == jnp.arange(N_EXPERTS)[None, :]).astype(jnp.int32)
    csum = jnp.cumsum(onehot, axis=0)
    rank = jnp.sum((csum - onehot) * onehot, axis=1)
    counts = csum[-1]
    padded = (counts + MOE_TM - 1) // MOE_TM * MOE_TM
    pends = jnp.cumsum(padded)
    pstarts = pends - padded
    dest = (pstarts[e_flat] + rank).astype(jnp.int32)
    n_blocks = m // MOE_TM + N_EXPERTS
    m_pad = n_blocks * MOE_TM
    src = jnp.zeros((m_pad,), jnp.int32).at[dest].set(jnp.arange(m, dtype=jnp.int32) // 2)
    n_valid = (pends[-1] // MOE_TM).astype(jnp.int32).reshape(1)
    blk_start = jnp.arange(n_blocks, dtype=jnp.int32) * MOE_TM
    block_e = jnp.minimum(jnp.searchsorted(pends, blk_start, side="right"),
                          N_EXPERTS - 1).astype(jnp.int32)
    last_e = block_e[jnp.maximum(n_valid[0] - 1, 0)]
    block_e = jnp.where(jnp.arange(n_blocks) < n_valid[0], block_e, last_e)
    xs = jnp.take(h2, src, axis=0)
    out = _moe_ffn(xs, block_e, n_valid, wg, wu, wd)
    pair = jnp.take(out, dest, axis=0).reshape(n, 2, D)
    return _combine(x, pair[:, 0], pair[:, 1], route, mods, fg, final)


def _blockdiag(w):
    eye = jnp.eye(H_A, dtype=w.dtype)
    return jnp.einsum("hij,hg->higj", w, eye).reshape(D_A, D_A)


def kernel(x_prompt, x_sample, state_rglru_h, cache_rglru_conv, state_mlstm_c, state_mlstm_n,
           state_mlstm_m, cache_mlstm_conv, cache_conformer_conv, c_prompt, c_sample,
           norm_g, ada_w, ada_b, final_norm_g, w_in_even, rglru_conv_w, rglru_conv_b,
           rglru_wa, rglru_ba, rglru_wx, rglru_bx, rglru_lambda, mlstm_conv_w, mlstm_conv_b,
           mlstm_gate_b, mlstm_norm_g, w_out_even, ffn_w_gate, ffn_w_up, ffn_w_down,
           conf_w_pw1, conf_b_pw1, conf_dw_w, conf_dw_b, conf_ln_g, conf_ln_b, conf_w_pw2,
           conf_b_pw2, moe_router_w, moe_router_b, moe_w_gate, moe_w_up, moe_w_down):
    bp, tp, _ = x_prompt.shape
    bs, ts, _ = x_sample.shape
    n_p, n_s = bp * tp, bs * ts
    assert n_s == TM and tp % TM == 0
    p_tiles = n_p // TM
    depth = norm_g.shape[0]
    ts_pad = 8
    chunk = min(256, tp)

    x = jnp.concatenate([x_prompt.reshape(n_p, D), x_sample.reshape(n_s, D)], axis=0)
    c_all = jnp.concatenate([c_prompt, c_sample], axis=0)
    mod_all = _ada(c_all, ada_w, ada_b)
    zeros_row = jnp.zeros((depth, 1, 6 * D), F32)
    mod_a = jnp.concatenate([mod_all[:, :bp], zeros_row], axis=1)[:, :, None, :]
    mod_b = jnp.stack([jnp.zeros((depth, TM, 6 * D), F32),
                       jnp.repeat(mod_all[:, bp:], ts, axis=1)], axis=1)

    lay_p = _prompt_layout(bp, tp, chunk)
    lay_s = _sample_layout(bs, ts, ts_pad)

    def sample_rows(a2d):
        a = a2d.reshape(bs, ts, a2d.shape[-1])
        return jnp.pad(a, ((0, 0), (0, ts_pad - ts), (0, 0)))

    def sample_flat(a3d):
        return a3d[:, :ts].reshape(n_s, a3d.shape[-1])

    zero_state_p = dict(
        h=jnp.zeros((bp, 1, D_A), F32), rconv=jnp.zeros((bp, 3, D_A), F32),
        c=jnp.zeros((bp, H_B, DK, DK), F32), n=jnp.zeros((bp, H_B, DK), F32),
        m=jnp.zeros((bp, 1, 128), F32), mconv=jnp.zeros((bp, 3, 2 * D_B), F32),
        cconv=jnp.zeros((bp, CONV_C - 1, D), F32))

    outs_p = {k: [] for k in ("h", "rconv", "c", "n", "m", "mconv", "cconv")}
    outs_s = {k: [] for k in outs_p}

    for l in range(depth):
        mods = _Mods(mod_a[l], mod_b[l], tp // TM, p_tiles, bp)
        g1 = norm_g[l, 0].reshape(1, D)
        g2 = norm_g[l, 1].reshape(1, D)
        if l % 2 == 0:
            e = l // 2
            w_in = jnp.pad(w_in_even[e], ((0, 0), (0, D_Z - w_in_even.shape[2]))).astype(BF16)
            z = _inproj(x, g1, mods, w_in, jnp.zeros((1, D_Z), F32), glu=False)
            z_s = sample_rows(z[n_p:])
            cw_a = rglru_conv_w[e]
            cb_a = rglru_conv_b[e].reshape(1, D_A)
            wcat = jnp.concatenate([_blockdiag(rglru_wa[e]), _blockdiag(rglru_wx[e])],
                                   axis=1).astype(BF16)
            bcat = jnp.concatenate([rglru_ba[e], rglru_bx[e]]).reshape(1, 2 * D_A)
            lam = rglru_lambda[e].reshape(1, D_A)
            ya_p, h_p, rc_p = _rglru(z, lay_p, zero_state_p["rconv"], zero_state_p["h"],
                                     cw_a, cb_a, wcat, bcat, lam, True)
            ya_s, h_s, rc_s = _rglru(z_s, lay_s, cache_rglru_conv[e],
                                     state_rglru_h[e][:, None, :], cw_a, cb_a, wcat, bcat, lam,
                                     False)
            cw_b = mlstm_conv_w[e]
            cb_b = mlstm_conv_b[e].reshape(1, 2 * D_B)
            gb = jnp.pad(mlstm_gate_b[e], (0, 128 - 2 * H_B)).reshape(1, 128)
            ng = mlstm_norm_g[e].reshape(1, D_B)
            yb_p, c_p, nn_p, m_p, mc_p = _mlstm(z, lay_p, zero_state_p["mconv"],
                                                zero_state_p["c"], zero_state_p["n"],
                                                zero_state_p["m"], cw_b, cb_b, gb, ng)
            m0_s = jnp.pad(state_mlstm_m[e], ((0, 0), (0, 128 - H_B)))[:, None, :]
            yb_s, c_s, nn_s, m_s, mc_s = _mlstm(z_s, lay_s, cache_mlstm_conv[e],
                                                state_mlstm_c[e], state_mlstm_n[e], m0_s,
                                                cw_b, cb_b, gb, ng)
            for outs, vals in ((outs_p, (h_p, rc_p, c_p, nn_p, m_p, mc_p)),
                               (outs_s, (h_s, rc_s, c_s, nn_s, m_s, mc_s))):
                outs["h"].append(vals[0][:, 0, :])
                outs["rconv"].append(vals[1])
                outs["c"].append(vals[2])
                outs["n"].append(vals[3])
                outs["m"].append(vals[4][:, 0, :H_B])
                outs["mconv"].append(vals[5])
            x = _outproj([(ya_p, sample_flat(ya_s)), (yb_p, sample_flat(yb_s))], x, mods, 2,
                         w_out_even[e].astype(BF16), jnp.zeros((1, D), F32), p_tiles)
            x = _ffn(x, g2, mods, ffn_w_gate[e].astype(BF16), ffn_w_up[e].astype(BF16),
                     ffn_w_down[e].astype(BF16))
        else:
            o = l // 2
            u = _inproj(x, g1, mods, conf_w_pw1[o].astype(BF16),
                        conf_b_pw1[o].reshape(1, 2 * D), glu=True)
            u_s = sample_rows(u[n_p:])
            cargs = (conf_dw_w[o], conf_dw_b[o].reshape(1, D), conf_ln_g[o].reshape(1, D),
                     conf_ln_b[o].reshape(1, D))
            act_p, cc_p = _conf(u, lay_p, zero_state_p["cconv"], *cargs)
            act_s, cc_s = _conf(u_s, lay_s, cache_conformer_conv[o], *cargs)
            outs_p["cconv"].append(cc_p)
            outs_s["cconv"].append(cc_s)
            x = _outproj([(act_p, sample_flat(act_s))], x, mods, 2,
                         conf_w_pw2[o].astype(BF16), conf_b_pw2[o].reshape(1, D), p_tiles)
            wr = jnp.pad(moe_router_w[o], ((0, 0), (0, 128 - N_EXPERTS)))
            br = jnp.pad(moe_router_b[o], (0, 128 - N_EXPERTS)).reshape(1, 128)
            x = _moe(x, g2, mods, wr, br, moe_w_gate[o].astype(BF16), moe_w_up[o].astype(BF16),
                     moe_w_down[o].astype(BF16), final_norm_g.reshape(1, D), l == depth - 1)

    if depth % 2 == 1:
        raise NotImplementedError("final norm is fused into the last (odd) layer")
    y_prompt = x[:n_p].reshape(bp, tp, D)
    y_sample = x[n_p:].reshape(bs, ts, D)
    keys = ("h", "rconv", "c", "n", "m", "mconv", "cconv")
    return (y_prompt, y_sample,
            *[jnp.stack(outs_p[k]) for k in keys],
            *[jnp.stack(outs_s[k]) for k in keys])
```
